```python
import math
import jax, jax.numpy as jnp
from jax import lax
import numpy as np

D_MODEL = 1024
BATCH = 16
SEQ = 4096
DEPTH = 2
DEC_BATCH = 4
DEC_SEQ = 4096
PAST_LEN = 128

N_META = 16
MLA_HEADS = 16
MLA_Q_LORA = 768
MLA_KV_LORA = 256
MLA_D_NOPE = 64
MLA_D_ROPE = 32
MLA_D_V = 64
MLA_SCALE = (MLA_D_NOPE + MLA_D_ROPE) ** -0.5
ROPE_BASE = 10000.0
Q_BLOCK = 128
HG_HEADS = 8
HG_D_K = 128
HG_D_V = D_MODEL // HG_HEADS
HG_HK = HG_HEADS * HG_D_K
HG_HV = HG_HEADS * HG_D_V
HG_CHUNK = 64
D_FF = 2816
CONV_WIDTH = 3
ALPHA = (2 * DEPTH) ** 0.25
BETA = (8 * DEPTH) ** -0.25
EPS = 1e-6
N_MLA_LAYERS = (DEPTH + 1) // 2
N_HGRN_LAYERS = DEPTH // 2

kernel_name = "hybrid_mla_hgrn2_convffn_encoder"


def rms_norm(x, gain):
    x32 = x.astype(jnp.float32)
    y = x32 * lax.rsqrt(jnp.mean(x32 * x32, axis=-1, keepdims=True) + EPS) * gain.astype(jnp.float32)
    return y.astype(x.dtype)


def layer_norm(x, gain, bias):
    x32 = x.astype(jnp.float32)
    mu = jnp.mean(x32, axis=-1, keepdims=True)
    xc = x32 - mu
    var = jnp.mean(xc * xc, axis=-1, keepdims=True)
    y = xc * lax.rsqrt(var + EPS) * gain.astype(jnp.float32) + bias.astype(jnp.float32)
    return y.astype(x.dtype)


def rope_tables(length):
    inv = 1.0 / (ROPE_BASE ** (jnp.arange(0, MLA_D_ROPE, 2, dtype=jnp.float32) / MLA_D_ROPE))
    ang = jnp.arange(length, dtype=jnp.float32)[:, None] * inv[None, :]
    return jnp.cos(ang), jnp.sin(ang)


def apply_rope(x, cos, sin):
    x32 = x.astype(jnp.float32)
    x1, x2 = jnp.split(x32, 2, axis=-1)
    return jnp.concatenate([x1 * cos - x2 * sin, x1 * sin + x2 * cos], axis=-1).astype(x.dtype)


def mla_attend(q_nope, q_rope, k_nope, k_rope, v):
    B, L, H, _ = q_nope.shape
    nb = -(-L // Q_BLOCK)
    pad = nb * Q_BLOCK - L
    def blocks(t):
        t = jnp.pad(t, ((0, 0), (0, pad), (0, 0), (0, 0)))
        return t.reshape(B, nb, Q_BLOCK, H, t.shape[-1]).transpose(1, 0, 2, 3, 4)
    def one_block(args):
        qn, qr = args
        s = jnp.einsum('bqhd,bkhd->bhqk', qn, k_nope) + jnp.einsum('bqhd,bkd->bhqk', qr, k_rope)
        p = jax.nn.softmax(s.astype(jnp.float32) * MLA_SCALE, axis=-1)
        return jnp.einsum('bhqk,bkhd->bqhd', p.astype(v.dtype), v)
    o = lax.map(one_block, (blocks(q_nope), blocks(q_rope)))
    return o.transpose(1, 0, 2, 3, 4).reshape(B, nb * Q_BLOCK, H, MLA_D_V)[:, :L]


def mla_mixer(h, w_in, q_norm, kv_norm, w_uq, w_ukv, w_o, cos, sin):
    B, L, _ = h.shape
    proj = h @ w_in
    c_q, c_kv, k_rope = jnp.split(proj, [MLA_Q_LORA, MLA_Q_LORA + MLA_KV_LORA], axis=-1)
    c_q = rms_norm(c_q, q_norm)
    c_kv = rms_norm(c_kv, kv_norm)
    q = (c_q @ w_uq).reshape(B, L, MLA_HEADS, MLA_D_NOPE + MLA_D_ROPE)
    kv = (c_kv @ w_ukv).reshape(B, L, MLA_HEADS, MLA_D_NOPE + MLA_D_V)
    q_nope, q_rope = jnp.split(q, [MLA_D_NOPE], axis=-1)
    k_nope, v = jnp.split(kv, [MLA_D_NOPE], axis=-1)
    q_rope = apply_rope(q_rope, cos[:, None, :], sin[:, None, :])
    k_rope = apply_rope(k_rope, cos, sin)
    o = mla_attend(q_nope, q_rope, k_nope, k_rope, v)
    return o.reshape(B, L, MLA_HEADS * MLA_D_V) @ w_o


def gla_chunked(q, k, v, logf):
    B, Lp, H, K = q.shape
    V = v.shape[-1]
    C = HG_CHUNK
    N = Lp // C
    ch = lambda t: t.reshape(B, N, C, H, t.shape[-1])
    q, k, v, logf = ch(q), ch(k), ch(v), ch(logf)
    b = jnp.cumsum(logf, axis=2)
    ref = b[:, :, C // 2:C // 2 + 1]
    a = jnp.einsum('bnthk,bnshk->bnhts', q * jnp.exp(b - ref), k * jnp.exp(ref - b))
    a = jnp.where(jnp.tril(jnp.ones((C, C), dtype=bool)), a, 0.0)
    o_intra = jnp.einsum('bnhts,bnshv->bnthv', a, v)
    b_last = b[:, :, -1]
    u = jnp.einsum('bnshk,bnshv->bnhkv', k * jnp.exp(b_last[:, :, None] - b), v)
    q_dec = q * jnp.exp(b)
    def step(s, xs):
        dec_n, u_n, q_n = xs
        o_n = jnp.einsum('bthk,bhkv->bthv', q_n, s)
        return jnp.exp(dec_n)[..., None] * s + u_n, o_n
    s0 = jnp.zeros((B, H, K, V), jnp.float32)
    _, o_inter = lax.scan(step, s0, (b_last.transpose(1, 0, 2, 3), u.transpose(1, 0, 2, 3, 4), q_dec.transpose(1, 0, 2, 3, 4)))
    o = o_intra + o_inter.transpose(1, 0, 2, 3, 4)
    return o.reshape(B, Lp, H, V)


def hgrn2_mixer(h, w_in, lb_param, o_norm, w_o, layer_idx):
    B, L, _ = h.shape
    pad = HG_CHUNK - N_META
    hp = jnp.pad(h, ((0, 0), (pad, 0), (0, 0)))
    Lp = L + pad
    proj = hp @ w_in
    q, i, f_fw, f_bw, g = jnp.split(proj, [HG_HK, HG_HK + HG_HV, 2 * HG_HK + HG_HV, 3 * HG_HK + HG_HV], axis=-1)
    p = jax.nn.softmax(lb_param.astype(jnp.float32), axis=1)
    lb = (jnp.cumsum(p, axis=1) - p[:, :1])[:, layer_idx]
    valid = (jnp.arange(Lp) >= pad)[None, :, None]
    def forget(f_logit, lb_d):
        f = lb_d + (1.0 - lb_d) * jax.nn.sigmoid(f_logit.astype(jnp.float32))
        return jnp.where(valid, f, 1.0)
    heads = lambda t: t.reshape(B, Lp, HG_HEADS, -1).astype(jnp.float32)
    qh, vh = heads(q), heads(i)
    ff = heads(forget(f_fw, lb[0]))
    fb = heads(forget(f_bw, lb[1]))
    rev = lambda t: jnp.flip(t, axis=1)
    o_fw = gla_chunked(qh, 1.0 - ff, vh, jnp.log(ff))
    o_bw = rev(gla_chunked(rev(qh), rev(1.0 - fb), rev(vh), rev(jnp.log(fb))))
    o = (o_fw + o_bw)[:, pad:]
    o = o * lax.rsqrt(jnp.mean(o * o, axis=-1, keepdims=True) + EPS) * o_norm.astype(jnp.float32).reshape(HG_HEADS, HG_D_V)
    o = o.reshape(B, L, HG_HV) * jax.nn.silu(g[:, pad:].astype(jnp.float32))
    return o.astype(h.dtype) @ w_o


def conv_ffn(h, w_up, conv_w, conv_b, w_down):
    L = h.shape[1]
    u = h @ w_up
    half = CONV_WIDTH // 2
    up = jnp.pad(u, ((0, 0), (half, half), (0, 0)))
    u = sum(up[:, j:j + L] * conv_w[j] for j in range(CONV_WIDTH)) + conv_b
    val, gate = jnp.split(u, 2, axis=-1)
    return (jax.nn.silu(gate) * val) @ w_down


def trunk(x, meta_tokens, mla_w_in, mla_q_norm, mla_kv_norm, mla_w_uq, mla_w_ukv, mla_w_o,
          hgrn_w_in, hgrn_lower_bound, hgrn_o_norm, hgrn_w_o,
          ffn_w_up, ffn_conv_w, ffn_conv_b, ffn_w_down, ln_gain, ln_bias):
    B = x.shape[0]
    meta = jnp.broadcast_to(meta_tokens[None].astype(x.dtype), (B, N_META, D_MODEL))
    h = jnp.concatenate([meta, x], axis=1)
    cos, sin = rope_tables(h.shape[1])
    for layer in range(DEPTH):
        j = layer // 2
        if layer % 2 == 0:
            mix = mla_mixer(h, mla_w_in[j], mla_q_norm[j], mla_kv_norm[j], mla_w_uq[j], mla_w_ukv[j], mla_w_o[j], cos, sin)
        else:
            mix = hgrn2_mixer(h, hgrn_w_in[j], hgrn_lower_bound, hgrn_o_norm[j], hgrn_w_o[j], layer)
        h = layer_norm(ALPHA * h + mix, ln_gain[layer, 0], ln_bias[layer, 0])
        h = layer_norm(ALPHA * h + conv_ffn(h, ffn_w_up[layer], ffn_conv_w[layer], ffn_conv_b[layer], ffn_w_down[layer]),
                       ln_gain[layer, 1], ln_bias[layer, 1])
    return h[:, N_META:]


def setup_inputs(seed: int = 0) -> dict:
    key = jax.random.key(seed)
    ks = jax.random.split(key, 24)
    f32 = jnp.float32
    def nrm(k, shape, fan_in, scale=1.0):
        return jax.random.normal(k, shape, f32) * (scale * fan_in ** -0.5)
    def gain(k, shape):
        return 1.0 + 0.02 * jax.random.normal(k, shape, f32)
    NA, NB = N_MLA_LAYERS, N_HGRN_LAYERS
    return {
        "x_prompt": jax.random.normal(ks[0], (BATCH, SEQ, D_MODEL), f32),
        "x_sample": jax.random.normal(ks[1], (DEC_BATCH, DEC_SEQ, D_MODEL), f32),
        "meta_tokens": jax.random.normal(ks[2], (N_META, D_MODEL), f32),
        "mla_w_in": nrm(ks[3], (NA, D_MODEL, MLA_Q_LORA + MLA_KV_LORA + MLA_D_ROPE), D_MODEL),
        "mla_q_norm": gain(ks[4], (NA, MLA_Q_LORA)),
        "mla_kv_norm": gain(ks[5], (NA, MLA_KV_LORA)),
        "mla_w_uq": nrm(ks[6], (NA, MLA_Q_LORA, MLA_HEADS * (MLA_D_NOPE + MLA_D_ROPE)), MLA_Q_LORA),
        "mla_w_ukv": nrm(ks[7], (NA, MLA_KV_LORA, MLA_HEADS * (MLA_D_NOPE + MLA_D_V)), MLA_KV_LORA),
        "mla_w_o": nrm(ks[8], (NA, MLA_HEADS * MLA_D_V, D_MODEL), MLA_HEADS * MLA_D_V, BETA),
        "hgrn_w_in": nrm(ks[9], (NB, D_MODEL, 3 * HG_HK + 2 * HG_HV), D_MODEL),
        "hgrn_lower_bound": 0.1 * jax.random.normal(ks[10], (2, DEPTH, HG_HK), f32),
        "hgrn_o_norm": gain(ks[11], (NB, HG_HV)),
        "hgrn_w_o": nrm(ks[12], (NB, HG_HV, D_MODEL), HG_HV, BETA),
        "ffn_w_up": nrm(ks[13], (DEPTH, D_MODEL, 2 * D_FF), D_MODEL),
        "ffn_conv_w": nrm(ks[14], (DEPTH, CONV_WIDTH, 2 * D_FF), CONV_WIDTH),
        "ffn_conv_b": 0.02 * jax.random.normal(ks[15], (DEPTH, 2 * D_FF), f32),
        "ffn_w_down": nrm(ks[16], (DEPTH, D_FF, D_MODEL), D_FF, BETA),
        "ln_gain": gain(ks[17], (DEPTH, 2, D_MODEL)),
        "ln_bias": 0.02 * jax.random.normal(ks[18], (DEPTH, 2, D_MODEL), f32),
    }


def reference(x_prompt, x_sample, meta_tokens, mla_w_in, mla_q_norm, mla_kv_norm, mla_w_uq, mla_w_ukv, mla_w_o,
              hgrn_w_in, hgrn_lower_bound, hgrn_o_norm, hgrn_w_o,
              ffn_w_up, ffn_conv_w, ffn_conv_b, ffn_w_down, ln_gain, ln_bias):
    y_prompt = trunk(x_prompt, meta_tokens, mla_w_in, mla_q_norm, mla_kv_norm, mla_w_uq, mla_w_ukv, mla_w_o,
                     hgrn_w_in, hgrn_lower_bound, hgrn_o_norm, hgrn_w_o,
                     ffn_w_up, ffn_conv_w, ffn_conv_b, ffn_w_down, ln_gain, ln_bias)
    y_sample = trunk(x_sample, meta_tokens, mla_w_in, mla_q_norm, mla_kv_norm, mla_w_uq, mla_w_ukv, mla_w_o,
                     hgrn_w_in, hgrn_lower_bound, hgrn_o_norm, hgrn_w_o,
                     ffn_w_up, ffn_conv_w, ffn_conv_b, ffn_w_down, ln_gain, ln_bias)
    return (y_prompt, y_sample)
```

```python
import functools

import jax
import jax.numpy as jnp
from jax import lax
from jax.experimental import pallas as pl
from jax.experimental.pallas import tpu as pltpu

F32 = jnp.float32
BF16 = jnp.bfloat16

D_MODEL = 1024
DEPTH = 2
N_META = 16
MLA_HEADS = 16
MLA_Q_LORA = 768
MLA_KV_LORA = 256
MLA_D_NOPE = 64
MLA_D_ROPE = 32
MLA_D_V = 64
MLA_SCALE = (MLA_D_NOPE + MLA_D_ROPE) ** -0.5
ROPE_BASE = 10000.0
HG_HEADS = 8
HG_D_K = 128
HG_CHUNK = 64
D_FF = 2816
ALPHA = (2 * DEPTH) ** 0.25
EPS = 1e-6

LANES = 128
PAD = 112
TOK0 = PAD + N_META
TM = 704
TQ = 352
TK = 704
FC = 256
HALO = 16
HEAD_W = MLA_HEADS * LANES
BIAS_LANE = MLA_D_NOPE + MLA_D_ROPE
KEY_MASK = -1e30
VMEM_LIMIT = 56 * 1024 * 1024


def _cparams(sem):
    return pltpu.CompilerParams(dimension_semantics=sem, vmem_limit_bytes=VMEM_LIMIT)


def _row_pos(tile_idx, tiles_per_seq, rows):
    base = (tile_idx % tiles_per_seq) * rows
    return base + lax.broadcasted_iota(jnp.int32, (rows, 1), 0)


def _layer_norm_rows(y, gain, bias):
    mu = jnp.mean(y, axis=-1, keepdims=True)
    yc = y - mu
    var = jnp.mean(yc * yc, axis=-1, keepdims=True)
    return yc * lax.rsqrt(var + EPS) * gain + bias


def _sigmoid(x):
    return 1.0 / (1.0 + jnp.exp(-x))


def _rope_group(x, c, s_lo, s_hi):
    return x * c + pltpu.roll(x, MLA_D_ROPE // 2, 1) * s_hi + pltpu.roll(x, LANES - MLA_D_ROPE // 2, 1) * s_lo


def _mla_pre_kernel(h_ref, tab_ref, w_in_ref, qn_ref, kvn_ref, w_uq_ref, w_k_ref, w_v_ref,
                    q_ref, k_ref, v_ref, *, tiles_per_seq):
    rows = h_ref.shape[0]
    pos = _row_pos(pl.program_id(0), tiles_per_seq, rows)
    hb = h_ref[...].astype(BF16)
    proj = jnp.dot(hb, w_in_ref[...], preferred_element_type=F32)
    cq = proj[:, :MLA_Q_LORA]
    ckv = proj[:, MLA_Q_LORA:MLA_Q_LORA + MLA_KV_LORA]
    kr = proj[:, MLA_Q_LORA + MLA_KV_LORA:]
    cq = cq * lax.rsqrt(jnp.mean(cq * cq, axis=-1, keepdims=True) + EPS) * qn_ref[...]
    ckv = ckv * lax.rsqrt(jnp.mean(ckv * ckv, axis=-1, keepdims=True) + EPS) * kvn_ref[...]
    cqb = cq.astype(BF16)
    ckvb = ckv.astype(BF16)

    c = tab_ref[:, 0:LANES]
    s_lo = tab_ref[:, LANES:2 * LANES]
    s_hi = tab_ref[:, 2 * LANES:3 * LANES]
    lane = lax.broadcasted_iota(jnp.int32, (rows, LANES), 1)
    is_bias = lane == BIAS_LANE

    v_ref[...] = jnp.dot(ckvb, w_v_ref[...], preferred_element_type=F32).astype(BF16)

    kbias = jnp.where(pos < PAD, KEY_MASK, 0.0)
    kr_rot = jnp.where(is_bias, kbias, _rope_group(kr, c, s_lo, s_hi))
    q_raw = jnp.dot(cqb, w_uq_ref[...], preferred_element_type=F32)
    k_raw = jnp.dot(ckvb, w_k_ref[...], preferred_element_type=F32)
    for g in range(MLA_HEADS):
        sl = slice(g * LANES, (g + 1) * LANES)
        qg = _rope_group(q_raw[:, sl], c, s_lo, s_hi) * MLA_SCALE
        q_ref[:, sl] = jnp.where(is_bias, 1.0, qg).astype(BF16)
        k_ref[:, sl] = (k_raw[:, sl] + kr_rot).astype(BF16)


def _mla_pre(h, tab, w_in, q_norm, kv_norm, w_uq, w_k, w_v, tiles_per_seq):
    rows = h.shape[0]
    n = rows // TM
    full = lambda a: pl.BlockSpec(a.shape, lambda i: (0, 0))
    return pl.pallas_call(
        functools.partial(_mla_pre_kernel, tiles_per_seq=tiles_per_seq),
        grid=(n,),
        in_specs=[
            pl.BlockSpec((TM, D_MODEL), lambda i: (i, 0)),
            pl.BlockSpec((TM, 3 * LANES), lambda i: (i % tiles_per_seq, 0)),
            full(w_in), full(q_norm), full(kv_norm), full(w_uq), full(w_k), full(w_v),
        ],
        out_specs=[
            pl.BlockSpec((TM, HEAD_W), lambda i: (i, 0)),
            pl.BlockSpec((TM, HEAD_W), lambda i: (i, 0)),
            pl.BlockSpec((TM, MLA_HEADS * MLA_D_V), lambda i: (i, 0)),
        ],
        out_shape=[
            jax.ShapeDtypeStruct((rows, HEAD_W), BF16),
            jax.ShapeDtypeStruct((rows, HEAD_W), BF16),
            jax.ShapeDtypeStruct((rows, MLA_HEADS * MLA_D_V), BF16),
        ],
        compiler_params=_cparams(("parallel",)),
        name="mla_pre",
    )(h, tab, w_in, q_norm, kv_norm, w_uq, w_k, w_v)


def _attn_kernel(q_ref, k_ref, v_ref, o_ref):
    tq = q_ref.shape[0]
    n_kv = k_ref.shape[0] // TK
    outs = []
    for hh in range(2):
        hs = slice(hh * LANES, (hh + 1) * LANES)
        q = q_ref[:, hs]
        m = jnp.full((tq, 1), -jnp.inf, F32)
        l = jnp.zeros((tq, 1), F32)
        acc = jnp.zeros((tq, LANES), F32)
        for j in range(n_kv):
            ks = slice(j * TK, (j + 1) * TK)
            s = lax.dot_general(q, k_ref[ks, hs], (((1,), (1,)), ((), ())),
                                preferred_element_type=F32)
            m_new = jnp.maximum(m, jnp.max(s, axis=-1, keepdims=True))
            alpha = jnp.exp(m - m_new)
            p = jnp.exp(s - m_new)
            l = alpha * l + jnp.sum(p, axis=-1, keepdims=True)
            acc = alpha * acc + jnp.dot(p.astype(BF16), v_ref[ks, :], preferred_element_type=F32)
            m = m_new
        outs.append(acc / l)
    lane = lax.broadcasted_iota(jnp.int32, (tq, LANES), 1)
    o_ref[...] = jnp.where(lane < MLA_D_V, outs[0], outs[1]).astype(BF16)


def _attention(q, k, v, batch, lp):
    rows = q.shape[0]
    nq = lp // TQ
    return pl.pallas_call(
        _attn_kernel,
        grid=(batch, MLA_HEADS // 2, nq),
        in_specs=[
            pl.BlockSpec((TQ, 2 * LANES), lambda b, h, i: (b * nq + i, h)),
            pl.BlockSpec((lp, 2 * LANES), lambda b, h, i: (b, h)),
            pl.BlockSpec((lp, LANES), lambda b, h, i: (b, h)),
        ],
        out_specs=pl.BlockSpec((TQ, LANES), lambda b, h, i: (b * nq + i, h)),
        out_shape=jax.ShapeDtypeStruct((rows, MLA_HEADS * MLA_D_V), BF16),
        compiler_params=_cparams(("parallel", "parallel", "parallel")),
        name="mla_attention",
    )(q, k, v)


def _residual_ln_store(o_ref, h, mix, gain, bias, pos):
    y = _layer_norm_rows(ALPHA * h + mix, gain, bias)
    o_ref[...] = jnp.where(pos >= PAD, y, 0.0)


def _mla_post_kernel(x_ref, h_ref, w_ref, g_ref, b_ref, o_ref, *, tiles_per_seq):
    pos = _row_pos(pl.program_id(0), tiles_per_seq, h_ref.shape[0])
    mix = jnp.dot(x_ref[...], w_ref[...], preferred_element_type=F32)
    _residual_ln_store(o_ref, h_ref[...], mix, g_ref[...], b_ref[...], pos)


def _mla_post(x, h, w_o, gain, bias, tiles_per_seq):
    rows = h.shape[0]
    full = lambda a: pl.BlockSpec(a.shape, lambda i: (0, 0))
    row = pl.BlockSpec((TM, D_MODEL), lambda i: (i, 0))
    return pl.pallas_call(
        functools.partial(_mla_post_kernel, tiles_per_seq=tiles_per_seq),
        grid=(rows // TM,),
        in_specs=[row, row, full(w_o), full(gain), full(bias)],
        out_specs=row,
        out_shape=jax.ShapeDtypeStruct((rows, D_MODEL), F32),
        compiler_params=_cparams(("parallel",)),
        name="mla_post",
    )(x, h, w_o, gain, bias)


def _hgrn_post_kernel(ofw_ref, obw_ref, g_ref, h_ref, on_ref, w_ref, gain_ref, bias_ref, o_ref, *, tiles_per_seq):
    pos = _row_pos(pl.program_id(0), tiles_per_seq, h_ref.shape[0])
    o = ofw_ref[...] + obw_ref[...]
    parts = []
    for hd in range(HG_HEADS):
        oh = o[:, hd * HG_D_K:(hd + 1) * HG_D_K]
        parts.append(oh * lax.rsqrt(jnp.mean(oh * oh, axis=-1, keepdims=True) + EPS))
    g = g_ref[...]
    gated = jnp.concatenate(parts, axis=-1) * on_ref[...] * (g * _sigmoid(g))
    mix = jnp.dot(gated.astype(BF16), w_ref[...], preferred_element_type=F32)
    _residual_ln_store(o_ref, h_ref[...], mix, gain_ref[...], bias_ref[...], pos)


def _hgrn_post(o_fw, o_bw, proj, h, o_norm, w_o, gain, bias, tiles_per_seq):
    rows = h.shape[0]
    full = lambda a: pl.BlockSpec(a.shape, lambda i: (0, 0))
    row = pl.BlockSpec((TM, D_MODEL), lambda i: (i, 0))
    return pl.pallas_call(
        functools.partial(_hgrn_post_kernel, tiles_per_seq=tiles_per_seq),
        grid=(rows // TM,),
        in_specs=[row, row, pl.BlockSpec((TM, D_MODEL), lambda i: (i, 4)), row,
                  full(o_norm), full(w_o), full(gain), full(bias)],
        out_specs=row,
        out_shape=jax.ShapeDtypeStruct((rows, D_MODEL), F32),
        compiler_params=_cparams(("parallel",)),
        name="hgrn_post",
    )(o_fw, o_bw, proj, h, o_norm, w_o, gain, bias)


def _ffn_kernel(hp_ref, h_ref, hn_ref, wv_ref, wg_ref, cwv_ref, cwg_ref, cbv_ref, cbg_ref, wd_ref,
                gain_ref, bias_ref, o_ref, hcat_ref, acc_ref, *, tiles_per_seq):
    i = pl.program_id(0)
    j = pl.program_id(1)
    rows = h_ref.shape[0]
    ext = rows + 2 * HALO

    @pl.when(j == 0)
    def _():
        prev_ok = jnp.where(i > 0, 1.0, 0.0)
        next_ok = jnp.where(i < pl.num_programs(0) - 1, 1.0, 0.0)
        hcat_ref[0:HALO, :] = (hp_ref[...] * prev_ok).astype(BF16)
        hcat_ref[HALO:HALO + rows, :] = h_ref[...].astype(BF16)
        hcat_ref[HALO + rows:ext, :] = (hn_ref[...] * next_ok).astype(BF16)
        acc_ref[...] = jnp.zeros_like(acc_ref)

    hcat = hcat_ref[...]

    def conv(w_ref, cw_ref, cb_ref):
        u = jnp.dot(hcat, w_ref[...], preferred_element_type=F32)
        prev = pltpu.roll(u, 1, 0)[HALO:HALO + rows]
        nxt = pltpu.roll(u, ext - 1, 0)[HALO:HALO + rows]
        return prev * cw_ref[0:1, :] + u[HALO:HALO + rows] * cw_ref[1:2, :] + nxt * cw_ref[2:3, :] + cb_ref[...]

    val = conv(wv_ref, cwv_ref, cbv_ref)
    gate = conv(wg_ref, cwg_ref, cbg_ref)
    a = (gate * _sigmoid(gate) * val).astype(BF16)
    acc_ref[...] += jnp.dot(a, wd_ref[...], preferred_element_type=F32)

    @pl.when(j == pl.num_programs(1) - 1)
    def _():
        pos = _row_pos(i, tiles_per_seq, rows)
        _residual_ln_store(o_ref, h_ref[...], acc_ref[...], gain_ref[...], bias_ref[...], pos)


def _ffn(h, w_up, conv_w, conv_b, w_down, gain, bias, tiles_per_seq):
    rows = h.shape[0]
    n = rows // TM
    nf = D_FF // FC
    hb = TM // HALO
    last_halo = rows // HALO - 1
    full = lambda a: pl.BlockSpec(a.shape, lambda i, j: (0, 0))
    return pl.pallas_call(
        functools.partial(_ffn_kernel, tiles_per_seq=tiles_per_seq),
        grid=(n, nf),
        in_specs=[
            pl.BlockSpec((HALO, D_MODEL), lambda i, j: (jnp.maximum(i * hb - 1, 0), 0)),
            pl.BlockSpec((TM, D_MODEL), lambda i, j: (i, 0)),
            pl.BlockSpec((HALO, D_MODEL), lambda i, j: (jnp.minimum((i + 1) * hb, last_halo), 0)),
            pl.BlockSpec((D_MODEL, FC), lambda i, j: (0, j)),
            pl.BlockSpec((D_MODEL, FC), lambda i, j: (0, nf + j)),
            pl.BlockSpec((3, FC), lambda i, j: (0, j)),
            pl.BlockSpec((3, FC), lambda i, j: (0, nf + j)),
            pl.BlockSpec((1, FC), lambda i, j: (0, j)),
            pl.BlockSpec((1, FC), lambda i, j: (0, nf + j)),
            pl.BlockSpec((FC, D_MODEL), lambda i, j: (j, 0)),
            full(gain), full(bias),
        ],
        out_specs=pl.BlockSpec((TM, D_MODEL), lambda i, j: (i, 0)),
        out_shape=jax.ShapeDtypeStruct((rows, D_MODEL), F32),
        scratch_shapes=[pltpu.VMEM((TM + 2 * HALO, D_MODEL), BF16), pltpu.VMEM((TM, D_MODEL), F32)],
        compiler_params=_cparams(("parallel", "arbitrary")),
        name="conv_ffn",
    )(h, h, h, w_up, w_up, conv_w, conv_w, conv_b, conv_b, w_down, gain, bias)


def _hgrn_pre_kernel(h_ref, w_ref, lb_ref, o_ref, *, tiles_per_seq, layer_idx):
    j = pl.program_id(1)
    x = jnp.dot(h_ref[...].astype(BF16), w_ref[...], preferred_element_type=F32)

    @pl.when(jnp.logical_or(j < 2, j == 4))
    def _():
        o_ref[...] = x

    @pl.when(jnp.logical_and(j >= 2, j < 4))
    def _():
        lbs = []
        for d in range(2):
            a = [lb_ref[d * DEPTH + t:d * DEPTH + t + 1, :] for t in range(DEPTH)]
            mx = functools.reduce(jnp.maximum, a)
            e = [jnp.exp(t - mx) for t in a]
            tot = functools.reduce(lambda u, w: u + w, e)
            p = [t / tot for t in e]
            lbs.append(functools.reduce(lambda u, w: u + w, p[:layer_idx + 1]) - p[0])
        lb = jnp.where(j == 2, lbs[0], lbs[1])
        pos = _row_pos(pl.program_id(0), tiles_per_seq, h_ref.shape[0])
        f = lb + (1.0 - lb) * _sigmoid(x)
        o_ref[...] = jnp.where(pos >= PAD, f, 1.0)


def _hgrn_pre(h, w_in, lb_param, tiles_per_seq, layer_idx):
    rows = h.shape[0]
    nblk = w_in.shape[1] // D_MODEL
    return pl.pallas_call(
        functools.partial(_hgrn_pre_kernel, tiles_per_seq=tiles_per_seq, layer_idx=layer_idx),
        grid=(rows // TM, nblk),
        in_specs=[
            pl.BlockSpec((TM, D_MODEL), lambda i, j: (i, 0)),
            pl.BlockSpec((D_MODEL, D_MODEL), lambda i, j: (0, j)),
            pl.BlockSpec(lb_param.shape, lambda i, j: (0, 0)),
        ],
        out_specs=pl.BlockSpec((TM, D_MODEL), lambda i, j: (i, j)),
        out_shape=jax.ShapeDtypeStruct((rows, w_in.shape[1]), F32),
        compiler_params=_cparams(("parallel", "arbitrary")),
        name="hgrn_pre",
    )(h, w_in, lb_param)


def _gla_kernel(q_ref, v_ref, f_ref, o_ref, st_ref, *, reverse):
    rows = q_ref.shape[0]
    n_chunks = rows // HG_CHUNK
    C = HG_CHUNK

    @pl.when(pl.program_id(1) == 0)
    def _():
        st_ref[...] = jnp.zeros_like(st_ref)

    f = f_ref[...]
    q = q_ref[...]
    k = 1.0 - f
    r = lax.broadcasted_iota(jnp.int32, (rows, 1), 0) % C
    b = jnp.log(f)
    d = 1
    while d < C:
        if reverse:
            b = b + jnp.where(r < C - d, pltpu.roll(b, rows - d, 0), 0.0)
        else:
            b = b + jnp.where(r >= d, pltpu.roll(b, d, 0), 0.0)
        d *= 2
    mid = C - 1 - C // 2 if reverse else C // 2
    last = 0 if reverse else C - 1
    ti = lax.broadcasted_iota(jnp.int32, (C, C), 0)
    si = lax.broadcasted_iota(jnp.int32, (C, C), 1)
    keep = (ti <= si) if reverse else (ti >= si)

    order = range(n_chunks - 1, -1, -1) if reverse else range(n_chunks)
    for c in order:
        rs = slice(c * C, (c + 1) * C)
        bc = b[rs]
        qc = q[rs]
        kc = k[rs]
        b_mid = bc[mid:mid + 1]
        b_last = bc[last:last + 1]
        qa = (qc * jnp.exp(bc - b_mid)).astype(BF16)
        ka = (kc * jnp.exp(b_mid - bc)).astype(BF16)
        ku = (kc * jnp.exp(b_last - bc)).astype(BF16)
        qd = (qc * jnp.exp(bc)).astype(BF16)
        dec = jnp.exp(b_last)
        vc = v_ref[rs, :].astype(BF16)
        for hd in range(HG_HEADS):
            hs = slice(hd * HG_D_K, (hd + 1) * HG_D_K)
            a = lax.dot_general(qa[:, hs], ka[:, hs], (((1,), (1,)), ((), ())), preferred_element_type=F32)
            a = jnp.where(keep, a, 0.0).astype(BF16)
            o_intra = jnp.dot(a, vc[:, hs], preferred_element_type=F32)
            st = st_ref[hd]
            o_inter = lax.dot_general(qd[:, hs], st.astype(BF16), (((1,), (1,)), ((), ())),
                                      preferred_element_type=F32)
            u_t = jnp.dot(vc[:, hs].T, ku[:, hs], preferred_element_type=F32)
            st_ref[hd] = st * dec[:, hs] + u_t
            o_ref[rs, hs] = o_intra + o_inter


def _gla(proj, batch, lp, reverse):
    rows = proj.shape[0]
    nt = lp // TM
    f_col = 3 if reverse else 2
    tile = (lambda t: nt - 1 - t) if reverse else (lambda t: t)
    return pl.pallas_call(
        functools.partial(_gla_kernel, reverse=reverse),
        grid=(batch, nt),
        in_specs=[
            pl.BlockSpec((TM, D_MODEL), lambda b, t: (b * nt + tile(t), 0)),
            pl.BlockSpec((TM, D_MODEL), lambda b, t: (b * nt + tile(t), 1)),
            pl.BlockSpec((TM, D_MODEL), lambda b, t: (b * nt + tile(t), f_col)),
        ],
        out_specs=pl.BlockSpec((TM, D_MODEL), lambda b, t: (b * nt + tile(t), 0)),
        out_shape=jax.ShapeDtypeStruct((rows, D_MODEL), F32),
        scratch_shapes=[pltpu.VMEM((HG_HEADS, HG_D_K, HG_D_K), F32)],
        compiler_params=_cparams(("parallel", "arbitrary")),
        name="gla_bw" if reverse else "gla_fw",
    )(proj, proj, proj)


def _rope_table(lp):
    half = MLA_D_ROPE // 2
    inv = 1.0 / (ROPE_BASE ** (jnp.arange(0, MLA_D_ROPE, 2, dtype=F32) / MLA_D_ROPE))
    idx = jnp.maximum(jnp.arange(lp, dtype=jnp.int32) - PAD, 0).astype(F32)
    ang = idx[:, None] * inv[None, :]
    cos, sin = jnp.cos(ang), jnp.sin(ang)
    z = lambda w: jnp.zeros((lp, w), F32)
    keep = jnp.concatenate([jnp.ones((lp, MLA_D_NOPE), F32), cos, cos, z(LANES - BIAS_LANE)], axis=1)
    s_lo = jnp.concatenate([z(MLA_D_NOPE), -sin, z(half), z(LANES - BIAS_LANE)], axis=1)
    s_hi = jnp.concatenate([z(MLA_D_NOPE), z(half), sin, z(LANES - BIAS_LANE)], axis=1)
    return jnp.concatenate([keep, s_lo, s_hi], axis=1)


def _mla_weights(w_in, w_uq, w_ukv):
    dq = MLA_D_NOPE + MLA_D_ROPE
    lat = MLA_Q_LORA + MLA_KV_LORA
    w_in_p = jnp.concatenate([
        w_in[:, :lat],
        jnp.zeros((D_MODEL, MLA_D_NOPE), F32), w_in[:, lat:], jnp.zeros((D_MODEL, LANES - dq), F32)], axis=1)
    uq = w_uq.reshape(MLA_Q_LORA, MLA_HEADS, dq)
    uq = jnp.pad(uq, ((0, 0), (0, 0), (0, LANES - dq))).reshape(MLA_Q_LORA, HEAD_W)
    ukv = w_ukv.reshape(MLA_KV_LORA, MLA_HEADS, MLA_D_NOPE + MLA_D_V)
    uk = jnp.pad(ukv[:, :, :MLA_D_NOPE], ((0, 0), (0, 0), (0, LANES - MLA_D_NOPE))).reshape(MLA_KV_LORA, HEAD_W)
    uv = ukv[:, :, MLA_D_NOPE:].reshape(MLA_KV_LORA, MLA_HEADS * MLA_D_V)
    return w_in_p.astype(BF16), uq.astype(BF16), uk.astype(BF16), uv.astype(BF16)


def kernel(x_prompt, x_sample, meta_tokens, mla_w_in, mla_q_norm, mla_kv_norm, mla_w_uq, mla_w_ukv, mla_w_o,
           hgrn_w_in, hgrn_lower_bound, hgrn_o_norm, hgrn_w_o,
           ffn_w_up, ffn_conv_w, ffn_conv_b, ffn_w_down, ln_gain, ln_bias):
    bp, seq, _ = x_prompt.shape
    bs = x_sample.shape[0]
    assert x_sample.shape[1] == seq
    batch = bp + bs
    lp = TOK0 + seq
    assert lp % TM == 0 and lp % TQ == 0 and lp % TK == 0
    tps = lp // TM
    rows = batch * lp

    x = jnp.concatenate([x_prompt, x_sample], axis=0)
    head = jnp.concatenate([jnp.zeros((PAD, D_MODEL), F32), meta_tokens.astype(F32)], axis=0)
    h = jnp.concatenate([jnp.broadcast_to(head[None], (batch, TOK0, D_MODEL)), x], axis=1).reshape(rows, D_MODEL)

    tab = _rope_table(lp)
    row2 = lambda a: a.reshape(1, -1).astype(F32)

    for layer in range(DEPTH):
        jl = layer // 2
        if layer % 2 == 0:
            w_in, w_uq, w_k, w_v = _mla_weights(mla_w_in[jl], mla_w_uq[jl], mla_w_ukv[jl])
            q, k, v = _mla_pre(h, tab, w_in, row2(mla_q_norm[jl]), row2(mla_kv_norm[jl]), w_uq, w_k, w_v, tps)
            o = _attention(q, k, v, batch, lp)
            h = _mla_post(o, h, mla_w_o[jl].astype(BF16), row2(ln_gain[layer, 0]), row2(ln_bias[layer, 0]), tps)
        else:
            lb_param = hgrn_lower_bound.reshape(2 * DEPTH, HG_HEADS * HG_D_K).astype(F32)
            proj = _hgrn_pre(h, hgrn_w_in[jl].astype(BF16), lb_param, tps, layer)
            o_fw = _gla(proj, batch, lp, reverse=False)
            o_bw = _gla(proj, batch, lp, reverse=True)
            h = _hgrn_post(o_fw, o_bw, proj, h, row2(hgrn_o_norm[jl]), hgrn_w_o[jl].astype(BF16),
                           row2(ln_gain[layer, 0]), row2(ln_bias[layer, 0]), tps)
        h = _ffn(h, ffn_w_up[layer].astype(BF16), ffn_conv_w[layer].astype(F32), row2(ffn_conv_b[layer]),
                 ffn_w_down[layer].astype(BF16), row2(ln_gain[layer, 1]), row2(ln_bias[layer, 1]), tps)

    y = h.reshape(batch, lp, D_MODEL)[:, TOK0:]
    return (y[:bp], y[bp:])
```
